```python
import math
import jax
import jax.numpy as jnp
from jax import lax
import numpy as np

D_MODEL = 2048
BATCH = 8
SEQ = 4096
DEPTH = 4

N_MIXERS = 2
N_LAYERS_A = (DEPTH + 1) // 2
N_LAYERS_B = DEPTH // 2
CHUNK = 128
SGU_WIDTH = D_MODEL
SGU_GROUP = 128
SGU_HEADS = SGU_WIDTH // SGU_GROUP
SSM_WIDTH = D_MODEL
SSM_GROUP = 16
SSM_HEADS = SSM_WIDTH // SSM_GROUP
SSM_STATE = 64
DT_MIN = 1e-3
DT_MAX = 1e-1
FFN_HIDDEN = 5632
CONV_WIDTH = 3
EPS = 1e-6

kernel_name = 'hybrid_sgu_s5_convffn'


def rms_norm(x, g):
    xf = x.astype(jnp.float32)
    y = xf * lax.rsqrt(jnp.mean(xf * xf, axis=-1, keepdims=True) + EPS)
    return (y * g.astype(jnp.float32)).astype(x.dtype)


def chunked_sgu_mixer(h, w_in, g_v, w_s, b_s, w_out):
    bsz, seq, _ = h.shape
    z = jax.nn.gelu(h @ w_in)
    u, v = jnp.split(z, 2, axis=-1)
    v = rms_norm(v, g_v).reshape(bsz, seq // CHUNK, CHUNK, SGU_HEADS, SGU_GROUP)
    causal = jnp.tril(jnp.ones((CHUNK, CHUNK), dtype=bool))
    w = jnp.where(causal[None], w_s, jnp.zeros((), w_s.dtype))
    s = jnp.einsum('hts,bcshd->bcthd', w, v) + b_s.T[:, :, None]
    s = s.reshape(bsz, seq, SGU_WIDTH)
    return (u * s) @ w_out


def _cmul(ar, ai, br, bi):
    return ar * br - ai * bi, ar * bi + ai * br


def _scan_combine(earlier, later):
    a1r, a1i, b1r, b1i = earlier
    a2r, a2i, b2r, b2i = later
    ar, ai = _cmul(a2r, a2i, a1r, a1i)
    br, bi = _cmul(a2r, a2i, b1r, b1i)
    return ar, ai, br + b2r, bi + b2i


def s5_mixer(h, w_in, a_re, a_im, log_dt, b_re, b_im, c_re, c_im, d_skip, w_glu):
    f32 = jnp.float32
    bsz, seq, _ = h.shape
    n_chunks = seq // CHUNK
    u = (h @ w_in).astype(f32).reshape(bsz, n_chunks, CHUNK, SSM_HEADS, SSM_GROUP)
    u = u.transpose(1, 0, 2, 3, 4)
    dt = jnp.exp(log_dt.astype(f32))[:, None]
    lr, li = a_re.astype(f32), a_im.astype(f32)
    mag = jnp.exp(dt * lr)
    abar_r, abar_i = mag * jnp.cos(dt * li), mag * jnp.sin(dt * li)
    den = lr * lr + li * li
    qr = ((abar_r - 1.0) * lr + abar_i * li) / den
    qi = (abar_i * lr - (abar_r - 1.0) * li) / den
    bbar_r, bbar_i = _cmul(qr[..., None], qi[..., None], b_re.astype(f32), b_im.astype(f32))
    cr, ci = c_re.astype(f32), c_im.astype(f32)
    dd = d_skip.astype(f32).reshape(SSM_HEADS, SSM_GROUP)
    a_seq_r = jnp.broadcast_to(abar_r, (bsz, CHUNK, SSM_HEADS, SSM_STATE))
    a_seq_i = jnp.broadcast_to(abar_i, (bsz, CHUNK, SSM_HEADS, SSM_STATE))

    def chunk_step(carry, u_c):
        h0r, h0i = carry
        bur = jnp.einsum('gpc,btgc->btgp', bbar_r, u_c)
        bui = jnp.einsum('gpc,btgc->btgp', bbar_i, u_c)
        pr, pim, hr, hi = lax.associative_scan(
            _scan_combine, (a_seq_r, a_seq_i, bur, bui), axis=1)
        sr, si = _cmul(pr, pim, h0r[:, None], h0i[:, None])
        hr = hr + sr
        hi = hi + si
        y = (jnp.einsum('gcp,btgp->btgc', cr, hr)
             - jnp.einsum('gcp,btgp->btgc', ci, hi)
             + dd * u_c)
        return (hr[:, -1], hi[:, -1]), y

    init = (jnp.zeros((bsz, SSM_HEADS, SSM_STATE), f32),
            jnp.zeros((bsz, SSM_HEADS, SSM_STATE), f32))
    _, y = lax.scan(chunk_step, init, u)
    y = y.transpose(1, 0, 2, 3, 4).reshape(bsz, seq, SSM_WIDTH).astype(h.dtype)
    ga, gb = jnp.split(jax.nn.gelu(y) @ w_glu, 2, axis=-1)
    return ga * jax.nn.sigmoid(gb)


def conv_glu_ffn(h, w_up, conv_w, conv_b, w_down):
    seq = h.shape[1]
    z = h @ w_up
    zp = jnp.pad(z, ((0, 0), (CONV_WIDTH - 1, 0), (0, 0)))
    acc = conv_b + conv_w[CONV_WIDTH - 1] * zp[:, CONV_WIDTH - 1:CONV_WIDTH - 1 + seq]
    for k in range(CONV_WIDTH - 1):
        acc = acc + conv_w[k] * zp[:, k:k + seq]
    gate, val = jnp.split(acc, 2, axis=-1)
    return (jax.nn.silu(gate) * val) @ w_down


def setup_inputs(seed: int = 0) -> dict:
    key = jax.random.key(seed)
    ks = jax.random.split(key, 24)
    f32 = jnp.float32
    d = D_MODEL
    na, nb = N_LAYERS_A, N_LAYERS_B

    def nrm(k, shape, scale):
        return jax.random.normal(k, shape, f32) * scale

    def gain(k, shape):
        return 1.0 + 0.01 * jax.random.normal(k, shape, f32)

    n_idx = jnp.arange(SSM_STATE, dtype=f32)
    return {
        'x': nrm(ks[0], (BATCH, SEQ, d), 1.0),
        'norm_mix_g': gain(ks[1], (DEPTH, d)),
        'norm_ffn_g': gain(ks[2], (DEPTH, d)),
        'a_w_in': nrm(ks[3], (na, d, 2 * SGU_WIDTH), d ** -0.5),
        'a_g_v': gain(ks[4], (na, SGU_WIDTH)),
        'a_w_s': nrm(ks[5], (na, SGU_HEADS, CHUNK, CHUNK), 0.5 * CHUNK ** -0.5),
        'a_b_s': gain(ks[6], (na, SGU_HEADS, CHUNK)),
        'a_w_out': nrm(ks[7], (na, SGU_WIDTH, d), SGU_WIDTH ** -0.5),
        'b_w_in': nrm(ks[8], (nb, d, SSM_WIDTH), d ** -0.5),
        'b_a_re': -0.5 + nrm(ks[9], (nb, SSM_HEADS, SSM_STATE), 0.01),
        'b_a_im': math.pi * n_idx + nrm(ks[10], (nb, SSM_HEADS, SSM_STATE), 0.01),
        'b_log_dt': jax.random.uniform(ks[11], (nb, SSM_HEADS), f32,
                                       minval=math.log(DT_MIN), maxval=math.log(DT_MAX)),
        'b_b_re': nrm(ks[12], (nb, SSM_HEADS, SSM_STATE, SSM_GROUP), (2 * SSM_GROUP) ** -0.5),
        'b_b_im': nrm(ks[13], (nb, SSM_HEADS, SSM_STATE, SSM_GROUP), (2 * SSM_GROUP) ** -0.5),
        'b_c_re': nrm(ks[14], (nb, SSM_HEADS, SSM_GROUP, SSM_STATE), (2 * SSM_STATE) ** -0.5),
        'b_c_im': nrm(ks[15], (nb, SSM_HEADS, SSM_GROUP, SSM_STATE), (2 * SSM_STATE) ** -0.5),
        'b_d': nrm(ks[16], (nb, SSM_WIDTH), 1.0),
        'b_w_glu': nrm(ks[17], (nb, SSM_WIDTH, 2 * d), SSM_WIDTH ** -0.5),
        'f_w_up': nrm(ks[18], (DEPTH, d, 2 * FFN_HIDDEN), d ** -0.5),
        'f_conv_w': nrm(ks[19], (DEPTH, CONV_WIDTH, 2 * FFN_HIDDEN), CONV_WIDTH ** -0.5),
        'f_conv_b': nrm(ks[20], (DEPTH, 2 * FFN_HIDDEN), 0.01),
        'f_w_down': nrm(ks[21], (DEPTH, FFN_HIDDEN, d), FFN_HIDDEN ** -0.5),
        'final_g': gain(ks[22], (d,)),
    }


def reference(x, norm_mix_g, norm_ffn_g, a_w_in, a_g_v, a_w_s, a_b_s, a_w_out,
              b_w_in, b_a_re, b_a_im, b_log_dt, b_b_re, b_b_im, b_c_re, b_c_im, b_d, b_w_glu,
              f_w_up, f_conv_w, f_conv_b, f_w_down, final_g):
    h = x
    for i in range(DEPTH):
        j = i // N_MIXERS
        hn = rms_norm(h, norm_mix_g[i])
        if i % N_MIXERS == 0:
            h = h + chunked_sgu_mixer(hn, a_w_in[j], a_g_v[j], a_w_s[j], a_b_s[j], a_w_out[j])
        else:
            h = h + s5_mixer(hn, b_w_in[j], b_a_re[j], b_a_im[j], b_log_dt[j],
                             b_b_re[j], b_b_im[j], b_c_re[j], b_c_im[j], b_d[j], b_w_glu[j])
        h = h + conv_glu_ffn(rms_norm(h, norm_ffn_g[i]), f_w_up[i], f_conv_w[i],
                             f_conv_b[i], f_w_down[i])
    return rms_norm(h, final_g)
```

```python
import functools
import math

import jax
import jax.numpy as jnp
from jax import lax
from jax.experimental import pallas as pl
from jax.experimental.pallas import tpu as pltpu

F32 = jnp.float32
BF16 = jnp.bfloat16

EPS = 1e-6
CHUNK = 128
SGU_GROUP = 128
SSM_GROUP = 16
SUB = 16
CONV_WIDTH = 3
HALO = 16
V7X_VMEM_BYTES = 64 * 1024 * 1024
VMEM_LIMIT = V7X_VMEM_BYTES - 8 * 1024 * 1024


def _params(*sem):
    return pltpu.CompilerParams(dimension_semantics=sem, vmem_limit_bytes=VMEM_LIMIT)


def _dot(a, b):
    return jnp.dot(a, b, preferred_element_type=F32)


def _rms(x, g):
    return x * lax.rsqrt(jnp.mean(x * x, axis=-1, keepdims=True) + EPS) * g


def _gelu(x):
    c = math.sqrt(2.0 / math.pi)
    return 0.5 * x * (1.0 + jnp.tanh(c * (x + 0.044715 * (x * x * x))))


def _sigmoid(x):
    return 1.0 / (1.0 + jnp.exp(-x))


def _ffn_kernel(h_ref, halo_ref, g_ref, wg_ref, wv_ref, cwg_ref, cwv_ref, cbg_ref, cbv_ref,
                wd_ref, fg_ref, o_ref, hn_s, *, tm, seq, final):
    i = pl.program_id(0)
    j = pl.program_id(1)

    @pl.when(j == 0)
    def _():
        g = g_ref[...]
        hn_s[HALO:, :] = _rms(h_ref[...], g).astype(BF16)
        seq_start = (i * tm) % seq == 0
        hn_s[:HALO, :] = jnp.where(seq_start, 0.0, _rms(halo_ref[...], g)).astype(BF16)

    hn = hn_s[...]

    def conv(z, cw_ref, cb_ref):
        cw = cw_ref[...]
        acc = cb_ref[...] + cw[CONV_WIDTH - 1:CONV_WIDTH] * z[HALO:HALO + tm]
        for k in range(CONV_WIDTH - 1):
            lag = CONV_WIDTH - 1 - k
            acc = acc + cw[k:k + 1] * z[HALO - lag:HALO - lag + tm]
        return acc

    gate = conv(_dot(hn, wg_ref[...]), cwg_ref, cbg_ref)
    val = conv(_dot(hn, wv_ref[...]), cwv_ref, cbv_ref)
    act = (gate * _sigmoid(gate) * val).astype(BF16)
    contrib = _dot(act, wd_ref[...])

    @pl.when(j == 0)
    def _():
        o_ref[...] = h_ref[...] + contrib

    @pl.when(j > 0)
    def _():
        o_ref[...] += contrib

    if final:
        @pl.when(j == pl.num_programs(1) - 1)
        def _():
            o_ref[...] = _rms(o_ref[...], fg_ref[...])


def _ffn(h, g, w_up, conv_w, conv_b, w_down, final_g, *, seq, final, tm, tf):
    m, d = h.shape
    f = w_down.shape[0]
    tm = min(tm, seq)
    tf = min(tf, f)
    nf = f // tf
    assert m % tm == 0 and seq % tm == 0 and f % tf == 0 and tm % HALO == 0
    halo_blocks = tm // HALO
    kern = functools.partial(_ffn_kernel, tm=tm, seq=seq, final=final)
    return pl.pallas_call(
        kern,
        grid=(m // tm, nf),
        in_specs=[
            pl.BlockSpec((tm, d), lambda i, j: (i, 0)),
            pl.BlockSpec((HALO, d), lambda i, j: (jnp.maximum(i * halo_blocks - 1, 0), 0)),
            pl.BlockSpec((1, d), lambda i, j: (0, 0)),
            pl.BlockSpec((d, tf), lambda i, j: (0, j)),
            pl.BlockSpec((d, tf), lambda i, j: (0, j + nf)),
            pl.BlockSpec((CONV_WIDTH, tf), lambda i, j: (0, j)),
            pl.BlockSpec((CONV_WIDTH, tf), lambda i, j: (0, j + nf)),
            pl.BlockSpec((1, tf), lambda i, j: (0, j)),
            pl.BlockSpec((1, tf), lambda i, j: (0, j + nf)),
            pl.BlockSpec((tf, d), lambda i, j: (j, 0)),
            pl.BlockSpec((1, d), lambda i, j: (0, 0)),
        ],
        out_specs=pl.BlockSpec((tm, d), lambda i, j: (i, 0)),
        out_shape=jax.ShapeDtypeStruct((m, d), F32),
        scratch_shapes=[pltpu.VMEM((tm + HALO, d), BF16)],
        compiler_params=_params("parallel", "arbitrary"),
        name="ffn",
    )(h, h, g, w_up, w_up, conv_w, conv_w, conv_b, conv_b, w_down, final_g)


def _sgu_gate_kernel(h_ref, g_ref, win_ref, gv_ref, ws_ref, bs_ref, o_ref, *, tm, e):
    hn = _rms(h_ref[...], g_ref[...]).astype(BF16)
    z = _gelu(_dot(hn, win_ref[...]))
    u = z[:, :e]
    vn = _rms(z[:, e:], gv_ref[...]).astype(BF16)
    row = lax.broadcasted_iota(jnp.int32, (CHUNK, CHUNK), 0)
    col = lax.broadcasted_iota(jnp.int32, (CHUNK, CHUNK), 1)
    causal = row >= col
    for hd in range(e // SGU_GROUP):
        w = jnp.where(causal, ws_ref[hd], 0.0).astype(BF16)
        b = bs_ref[hd]
        cs = slice(hd * SGU_GROUP, (hd + 1) * SGU_GROUP)
        for c in range(tm // CHUNK):
            rs = slice(c * CHUNK, (c + 1) * CHUNK)
            s = _dot(w, vn[rs, cs]) + b
            o_ref[rs, cs] = (u[rs, cs] * s).astype(BF16)


def _sgu_gate(h, g, w_in, g_v, w_s, b_s, *, tm):
    m, d = h.shape
    e = g_v.shape[-1]
    heads = e // SGU_GROUP
    tm = min(tm, m)
    assert m % tm == 0 and tm % CHUNK == 0
    kern = functools.partial(_sgu_gate_kernel, tm=tm, e=e)
    const2 = lambda i: (0, 0)
    const3 = lambda i: (0, 0, 0)
    return pl.pallas_call(
        kern,
        grid=(m // tm,),
        in_specs=[
            pl.BlockSpec((tm, d), lambda i: (i, 0)),
            pl.BlockSpec((1, d), const2),
            pl.BlockSpec((d, 2 * e), const2, pipeline_mode=pl.Buffered(1)),
            pl.BlockSpec((1, e), const2),
            pl.BlockSpec((heads, CHUNK, CHUNK), const3),
            pl.BlockSpec((heads, CHUNK, SGU_GROUP), const3),
        ],
        out_specs=pl.BlockSpec((tm, e), lambda i: (i, 0)),
        out_shape=jax.ShapeDtypeStruct((m, e), BF16),
        compiler_params=_params("parallel"),
        name="sgu_gate",
    )(h, g, w_in, g_v, w_s, b_s)


def _matmul_res_kernel(a_ref, w_ref, r_ref, o_ref):
    o_ref[...] = r_ref[...] + _dot(a_ref[...], w_ref[...])


def _matmul_res(a, w, r, *, tm):
    m, k = a.shape
    n = w.shape[1]
    tm = min(tm, m)
    assert m % tm == 0
    return pl.pallas_call(
        _matmul_res_kernel,
        grid=(m // tm,),
        in_specs=[
            pl.BlockSpec((tm, k), lambda i: (i, 0)),
            pl.BlockSpec((k, n), lambda i: (0, 0), pipeline_mode=pl.Buffered(1)),
            pl.BlockSpec((tm, n), lambda i: (i, 0)),
        ],
        out_specs=pl.BlockSpec((tm, n), lambda i: (i, 0)),
        out_shape=jax.ShapeDtypeStruct((m, n), F32),
        compiler_params=_params("parallel"),
        name="matmul_res",
    )(a, w, r)


def _s5_in_kernel(h_ref, g_ref, wt_ref, o_ref, *, tl, d, n):
    g = g_ref[...]
    x = jnp.concatenate([h_ref[:, t * d:(t + 1) * d] for t in range(tl)], axis=0)
    hn = _rms(x, g).astype(BF16)
    ut = lax.dot_general(wt_ref[...], hn, (((1,), (1,)), ((), ())),
                         preferred_element_type=F32)
    e = ut.shape[0]
    for t in range(tl):
        o_ref[:, t] = ut[:, t * n:(t + 1) * n].reshape(e // SSM_GROUP, SSM_GROUP, n).astype(BF16)


def _s5_in(h3, g, w_in_t, *, tl):
    b, n, sd = h3.shape
    d = sd // SUB
    e = w_in_t.shape[0]
    grp = e // SSM_GROUP
    kern = functools.partial(_s5_in_kernel, tl=tl, d=d, n=n)
    return pl.pallas_call(
        kern,
        grid=(b, SUB // tl),
        in_specs=[
            pl.BlockSpec((None, n, tl * d), lambda i, t: (i, 0, t)),
            pl.BlockSpec((1, d), lambda i, t: (0, 0)),
            pl.BlockSpec((e, d), lambda i, t: (0, 0), pipeline_mode=pl.Buffered(1)),
        ],
        out_specs=pl.BlockSpec((None, grp, tl, SSM_GROUP, n), lambda i, t: (i, 0, t, 0, 0)),
        out_shape=jax.ShapeDtypeStruct((b, grp, SUB, SSM_GROUP, n), BF16),
        compiler_params=_params("parallel", "parallel"),
        name="s5_in",
    )(h3, g, w_in_t)


def _s5_core_kernel(x_ref, mt_ref, sg_ref, og_ref, cf_ref, y_ref, *, gb, n, p):
    lane = lax.broadcasted_iota(jnp.int32, (p, n), 1)

    def shifted(v, dist):
        return jnp.where(lane >= dist, pltpu.roll(v, dist, axis=1), 0.0)

    for gi in range(gb):
        x = x_ref[gi]
        z = _dot(sg_ref[gi], x)
        hr, hi = z[:p], z[p:]
        cf = cf_ref[gi]
        dist, k = 1, 0
        while dist < n:
            ar, ai = cf[:p, k:k + 1], cf[p:, k:k + 1]
            sr, si = shifted(hr, dist), shifted(hi, dist)
            hr, hi = hr + ar * sr - ai * si, hi + ar * si + ai * sr
            dist, k = dist * 2, k + 1
        hprev = jnp.concatenate([shifted(hr, 1), shifted(hi, 1)], axis=0).astype(BF16)
        y = _dot(mt_ref[gi], x) + _dot(og_ref[gi], hprev)
        y_ref[gi] = y.astype(BF16)


def _s5_core(xt, mt, sg, og, cf, *, gb):
    b, grp, r, n = xt.shape
    p2 = sg.shape[1]
    steps = cf.shape[-1]
    gb = min(gb, grp)
    assert grp % gb == 0
    kern = functools.partial(_s5_core_kernel, gb=gb, n=n, p=p2 // 2)
    return pl.pallas_call(
        kern,
        grid=(grp // gb, b),
        in_specs=[
            pl.BlockSpec((None, gb, r, n), lambda j, i: (i, j, 0, 0)),
            pl.BlockSpec((gb, r, r), lambda j, i: (j, 0, 0)),
            pl.BlockSpec((gb, p2, r), lambda j, i: (j, 0, 0)),
            pl.BlockSpec((gb, r, p2), lambda j, i: (j, 0, 0)),
            pl.BlockSpec((gb, p2, steps), lambda j, i: (j, 0, 0)),
        ],
        out_specs=pl.BlockSpec((None, gb, r, n), lambda j, i: (i, j, 0, 0)),
        out_shape=jax.ShapeDtypeStruct((b, grp, r, n), BF16),
        compiler_params=_params("parallel", "parallel"),
        name="s5_core",
    )(xt, mt, sg, og, cf)


def _s5_out_kernel(y_ref, h_ref, wa_ref, wb_ref, o_ref, *, tl, d, n):
    parts = []
    for t in range(tl):
        y = y_ref[:, t].astype(F32)
        y = y.reshape(y.shape[0] * SSM_GROUP, n)
        parts.append(_gelu(y).T.astype(BF16))
    yg = jnp.concatenate(parts, axis=0)
    ga = _dot(yg, wa_ref[...])
    gb = _dot(yg, wb_ref[...])
    res = ga * _sigmoid(gb)
    for t in range(tl):
        cs = slice(t * d, (t + 1) * d)
        o_ref[:, cs] = h_ref[:, cs] + res[t * n:(t + 1) * n]


def _s5_out(y5, h3, w_glu, *, tl):
    b, grp, _, _, n = y5.shape
    e = grp * SSM_GROUP
    d = w_glu.shape[1] // 2
    kern = functools.partial(_s5_out_kernel, tl=tl, d=d, n=n)
    return pl.pallas_call(
        kern,
        grid=(b, SUB // tl),
        in_specs=[
            pl.BlockSpec((None, grp, tl, SSM_GROUP, n), lambda i, t: (i, 0, t, 0, 0)),
            pl.BlockSpec((None, n, tl * d), lambda i, t: (i, 0, t)),
            pl.BlockSpec((e, d), lambda i, t: (0, 0), pipeline_mode=pl.Buffered(1)),
            pl.BlockSpec((e, d), lambda i, t: (0, 1), pipeline_mode=pl.Buffered(1)),
        ],
        out_specs=pl.BlockSpec((None, n, tl * d), lambda i, t: (i, 0, t)),
        out_shape=jax.ShapeDtypeStruct(h3.shape, F32),
        compiler_params=_params("parallel", "parallel"),
        name="s5_out",
    )(y5, h3, w_glu, w_glu)


def _s5_operators(a_re, a_im, log_dt, b_re, b_im, c_re, c_im, d_skip, n_blocks):
    grp, p = a_re.shape
    hp = lax.Precision.HIGHEST
    dt = jnp.exp(log_dt)[:, None]
    lr, li = a_re, a_im
    mag = jnp.exp(dt * lr)
    abar_r, abar_i = mag * jnp.cos(dt * li), mag * jnp.sin(dt * li)
    den = lr * lr + li * li
    qr = ((abar_r - 1.0) * lr + abar_i * li) / den
    qi = (abar_i * lr - (abar_r - 1.0) * li) / den
    bbar_r = qr[..., None] * b_re - qi[..., None] * b_im
    bbar_i = qr[..., None] * b_im + qi[..., None] * b_re

    def apow(k):
        k = k[:, None, None]
        m = jnp.exp(k * (dt * lr))
        return m * jnp.cos(k * (dt * li)), m * jnp.sin(k * (dt * li))

    kk = jnp.arange(SUB + 1, dtype=F32)
    pr, pi = apow(kk)

    cpr = c_re[None] * pr[:SUB, :, None, :] - c_im[None] * pi[:SUB, :, None, :]
    cpi = c_re[None] * pi[:SUB, :, None, :] + c_im[None] * pr[:SUB, :, None, :]
    kern = (jnp.einsum('kgcp,gpd->kgcd', cpr, bbar_r, precision=hp)
            - jnp.einsum('kgcp,gpd->kgcd', cpi, bbar_i, precision=hp))
    skip = d_skip.reshape(grp, SSM_GROUP)
    kern = kern.at[0].add(skip[:, :, None] * jnp.eye(SSM_GROUP, dtype=F32))
    t = jnp.arange(SUB)
    lag = t[:, None] - t[None, :]
    toep = jnp.where((lag >= 0)[:, :, None, None, None],
                     kern[jnp.clip(lag, 0, SUB - 1)], 0.0)
    mt = toep.transpose(2, 0, 3, 1, 4).reshape(grp, SUB * SSM_GROUP, SUB * SSM_GROUP)

    rr, ri = pr[SUB - 1 - t], pi[SUB - 1 - t]
    sr = rr[..., None] * bbar_r[None] - ri[..., None] * bbar_i[None]
    si = rr[..., None] * bbar_i[None] + ri[..., None] * bbar_r[None]
    sg = jnp.concatenate([sr, si], axis=2)
    sg = sg.transpose(1, 2, 0, 3).reshape(grp, 2 * p, SUB * SSM_GROUP)

    orr = c_re[None] * pr[1:, :, None, :] - c_im[None] * pi[1:, :, None, :]
    oi = c_re[None] * pi[1:, :, None, :] + c_im[None] * pr[1:, :, None, :]
    og = jnp.concatenate([orr, -oi], axis=3)
    og = og.transpose(1, 0, 2, 3).reshape(grp, SUB * SSM_GROUP, 2 * p)

    steps = max(1, (n_blocks - 1).bit_length())
    sc_r, sc_i = apow(SUB * (2.0 ** jnp.arange(steps, dtype=F32)))
    cf = jnp.concatenate([sc_r, sc_i], axis=2).transpose(1, 2, 0)
    return mt.astype(BF16), sg.astype(BF16), og.astype(BF16), cf


def kernel(x, norm_mix_g, norm_ffn_g, a_w_in, a_g_v, a_w_s, a_b_s, a_w_out,
           b_w_in, b_a_re, b_a_im, b_log_dt, b_b_re, b_b_im, b_c_re, b_c_im, b_d, b_w_glu,
           f_w_up, f_conv_w, f_conv_b, f_w_down, final_g):
    bsz, seq, d = x.shape
    depth = norm_mix_g.shape[0]
    m = bsz * seq
    n_blocks = seq // SUB
    h = x.reshape(m, d)
    row = lambda v: v.reshape(1, -1)

    for i in range(depth):
        j = i // 2
        if i % 2 == 0:
            e = a_g_v.shape[-1]
            b_full = jnp.broadcast_to(a_b_s[j][:, :, None], a_b_s[j].shape + (SGU_GROUP,))
            gated = _sgu_gate(h, row(norm_mix_g[i]), a_w_in[j].astype(BF16), row(a_g_v[j]),
                              a_w_s[j], b_full, tm=512)
            h = _matmul_res(gated, a_w_out[j].astype(BF16), h, tm=512)
        else:
            mt, sg, og, cf = _s5_operators(b_a_re[j], b_a_im[j], b_log_dt[j], b_b_re[j], b_b_im[j],
                                           b_c_re[j], b_c_im[j], b_d[j], n_blocks)
            grp = mt.shape[0]
            h3 = h.reshape(bsz, n_blocks, SUB * d)
            xt = _s5_in(h3, row(norm_mix_g[i]), b_w_in[j].T.astype(BF16), tl=2)
            y = _s5_core(xt.reshape(bsz, grp, SUB * SSM_GROUP, n_blocks), mt, sg, og, cf, gb=8)
            y5 = y.reshape(bsz, grp, SUB, SSM_GROUP, n_blocks)
            h = _s5_out(y5, h3, b_w_glu[j].astype(BF16), tl=2).reshape(m, d)
        h = _ffn(h, row(norm_ffn_g[i]), f_w_up[i].astype(BF16), f_conv_w[i], row(f_conv_b[i]),
                 f_w_down[i].astype(BF16), row(final_g), seq=seq, final=(i == depth - 1),
                 tm=512, tf=512)
    return h.reshape(bsz, seq, d)
```
